```python
import jax, jax.numpy as jnp
from jax import lax
import numpy as np

D_MODEL = 1024
BATCH = 8
SEQ = 2048
DEPTH = 2
DEC_BATCH = 4
DEC_SEQ = 8192
PAST_LEN = 128

HEAD_DIM = 64
D_MIX = D_MODEL
ATT_WIDTH = D_MIX // 4
ATT_HEADS = ATT_WIDTH // HEAD_DIM
ATT_KV_HEADS = ATT_HEADS // 2
ATT_GROUP = ATT_HEADS // ATT_KV_HEADS
WINDOW = 128
BLOCK = 128
REL_BUCKETS = 32
REL_MAX_DIST = 128
SSD_WIDTH = D_MIX // 2
SSD_HEADS = SSD_WIDTH // HEAD_DIM
SSD_STATE = 64
SSD_GROUPS = 2
SSD_CONV = 5
SSD_CHUNK = 128
SSD_CONV_DIM = SSD_WIDTH + 2 * SSD_GROUPS * SSD_STATE
MLSTM_WIDTH = D_MIX // 4
MLSTM_HEADS = MLSTM_WIDTH // HEAD_DIM
MLSTM_CHUNK = 128
N_EXPERTS = 16
EC_CAPACITY = 2
EXPERT_FF = D_MODEL // 2
RMS_EPS = 1e-6
IN_SPLITS = (ATT_WIDTH, ATT_KV_HEADS * HEAD_DIM, ATT_KV_HEADS * HEAD_DIM,
             SSD_WIDTH, SSD_CONV_DIM, 2 * SSD_HEADS,
             MLSTM_WIDTH, MLSTM_WIDTH, MLSTM_WIDTH, MLSTM_WIDTH, 2 * MLSTM_HEADS, 2 * MLSTM_HEADS)
D_IN = sum(IN_SPLITS)

kernel_name = 'hybrid_bidir_attn_ssd_mlstm_ec_moe'


def _rmsnorm(x, g):
    xf = x.astype(jnp.float32)
    y = xf * lax.rsqrt(jnp.mean(xf * xf, axis=-1, keepdims=True) + RMS_EPS)
    return (y * g.astype(jnp.float32)).astype(x.dtype)


def _flip(t):
    return t[:, ::-1]


def _t5_buckets(rel):
    half = REL_BUCKETS // 2
    exact = half // 2
    n = np.abs(rel)
    large = exact + (np.log(np.maximum(n, 1) / exact) / np.log(REL_MAX_DIST / exact)
                     * (half - exact)).astype(np.int32)
    large = np.minimum(large, half - 1)
    return (rel > 0).astype(np.int32) * half + np.where(n < exact, n, large)


def _window_attention(q, k, v, q_gain, k_gain, sink, rel_bias):
    B, S = q.shape[0], q.shape[1]
    nb = S // BLOCK
    q = _rmsnorm(q, q_gain)
    k = _rmsnorm(k, k_gain)
    qb = q.reshape(B, nb, BLOCK, ATT_KV_HEADS, ATT_GROUP, HEAD_DIM)
    pad = ((0, 0), (BLOCK, BLOCK), (0, 0), (0, 0))
    kp = jnp.pad(k, pad).reshape(B, nb + 2, BLOCK, ATT_KV_HEADS, HEAD_DIM)
    vp = jnp.pad(v, pad).reshape(B, nb + 2, BLOCK, ATT_KV_HEADS, HEAD_DIM)
    kw = jnp.concatenate([kp[:, :-2], kp[:, 1:-1], kp[:, 2:]], axis=2)
    vw = jnp.concatenate([vp[:, :-2], vp[:, 1:-1], vp[:, 2:]], axis=2)
    s = jnp.einsum('bnqhgd,bnkhd->bnhgqk', qb, kw,
                   preferred_element_type=jnp.float32) * (HEAD_DIM ** -0.5)
    kpos = np.arange(3 * BLOCK)[None, :] - BLOCK
    rel = kpos - np.arange(BLOCK)[:, None]
    bias = rel_bias.astype(jnp.float32)[_t5_buckets(rel)]
    bias = bias.transpose(2, 0, 1).reshape(ATT_KV_HEADS, ATT_GROUP, BLOCK, 3 * BLOCK)
    abs_k = np.arange(nb)[:, None] * BLOCK + kpos
    valid = (np.abs(rel) <= WINDOW)[None] & ((abs_k >= 0) & (abs_k < S))[:, None, :]
    s = jnp.where(valid[None, :, None, None], s + bias, -jnp.inf)
    sink_l = sink.astype(jnp.float32).reshape(ATT_KV_HEADS, ATT_GROUP)[None, None, :, :, None, None]
    m = jnp.maximum(jnp.max(s, axis=-1, keepdims=True), sink_l)
    p = jnp.exp(s - m)
    denom = jnp.sum(p, axis=-1, keepdims=True) + jnp.exp(sink_l - m)
    o = jnp.einsum('bnhgqk,bnkhd->bnqhgd', (p / denom).astype(v.dtype), vw)
    return o.reshape(B, S, ATT_WIDTH)


def _segsum(a):
    cs = jnp.cumsum(a, axis=-1)
    L = a.shape[-1]
    mask = np.tril(np.ones((L, L), dtype=bool))
    return jnp.where(mask, cs[..., :, None] - cs[..., None, :], -jnp.inf)


def _ssd_scan(x, dt, A, Bm, Cm):
    Bsz, S, H, P = x.shape
    N = Bm.shape[-1]
    nc = S // SSD_CHUNK
    xdt = (x * dt[..., None]).reshape(Bsz, nc, SSD_CHUNK, H, P)
    Bc = Bm.reshape(Bsz, nc, SSD_CHUNK, H, N)
    Cc = Cm.reshape(Bsz, nc, SSD_CHUNK, H, N)
    a = (dt * A).reshape(Bsz, nc, SSD_CHUNK, H).transpose(0, 3, 1, 2)
    a_cs = jnp.cumsum(a, axis=-1)
    cb = jnp.einsum('bclhn,bcshn->bhcls', Cc, Bc) * jnp.exp(_segsum(a))
    y_diag = jnp.einsum('bhcls,bcshp->bclhp', cb, xdt)
    decay_to_end = jnp.exp(a_cs[..., -1:] - a_cs)
    states = jnp.einsum('bclhn,bhcl,bclhp->bchpn', Bc, decay_to_end, xdt)
    chunk_decay = jnp.exp(a_cs[..., -1])

    def step(h, inp):
        st, dec = inp
        return h * dec[..., None, None] + st, h

    _, prev = lax.scan(step, jnp.zeros((Bsz, H, P, N), jnp.float32),
                       (states.transpose(1, 0, 2, 3, 4), chunk_decay.transpose(2, 0, 1)))
    prev = prev.transpose(1, 0, 2, 3, 4)
    y_off = jnp.einsum('bclhn,bchpn,bhcl->bclhp', Cc, prev, jnp.exp(a_cs))
    return (y_diag + y_off).reshape(Bsz, S, H, P)


def _ssd_mixer(z, xbc, dt_raw, conv_w, conv_b, dt_bias, a_log, d_skip, norm_g):
    B, S = z.shape[0], z.shape[1]
    half = SSD_CONV // 2
    xp = jnp.pad(xbc, ((0, 0), (half, half), (0, 0)))
    conv = conv_b + sum(xp[:, j:j + S] * conv_w[j] for j in range(SSD_CONV))
    xbc = jax.nn.silu(conv)
    xs, Bm, Cm = jnp.split(xbc, [SSD_WIDTH, SSD_WIDTH + SSD_GROUPS * SSD_STATE], axis=-1)
    rep = SSD_HEADS // SSD_GROUPS
    xs = xs.reshape(B, S, SSD_HEADS, HEAD_DIM).astype(jnp.float32)
    Bm = jnp.repeat(Bm.reshape(B, S, SSD_GROUPS, SSD_STATE), rep, axis=2).astype(jnp.float32)
    Cm = jnp.repeat(Cm.reshape(B, S, SSD_GROUPS, SSD_STATE), rep, axis=2).astype(jnp.float32)
    dt = jax.nn.softplus(dt_raw.astype(jnp.float32).reshape(B, S, 2, SSD_HEADS)
                         + dt_bias.astype(jnp.float32))
    A = -jnp.exp(a_log.astype(jnp.float32))
    y_f = _ssd_scan(xs, dt[:, :, 0], A[0], Bm, Cm)
    y_b = _flip(_ssd_scan(_flip(xs), _flip(dt[:, :, 1]), A[1], _flip(Bm), _flip(Cm)))
    y = y_f + y_b + xs * d_skip.astype(jnp.float32)[:, None]
    y = y.reshape(B, S, SSD_WIDTH) * jax.nn.silu(z.astype(jnp.float32))
    return _rmsnorm(y, norm_g).astype(z.dtype)


def _mlstm_scan(q, k, v, li, lf):
    Bsz, S, H, Dh = q.shape
    L = MLSTM_CHUNK
    nc = S // L
    qc = q.reshape(Bsz, nc, L, H, Dh)
    kc = k.reshape(Bsz, nc, L, H, Dh)
    vc = v.reshape(Bsz, nc, L, H, Dh)
    li = li.reshape(Bsz, nc, L, H).transpose(0, 3, 1, 2)
    lf = lf.reshape(Bsz, nc, L, H).transpose(0, 3, 1, 2)
    b = jnp.cumsum(lf, axis=-1)
    g = b[..., -1]
    a = g[..., None] - b + li
    m_loc = jnp.max(a, axis=-1)
    w = jnp.exp(a - m_loc[..., None])
    c_loc = jnp.einsum('bhcs,bcshe,bcshd->bched', w, vc, kc)
    n_loc = jnp.einsum('bhcs,bcshd->bchd', w, kc)

    def step(carry, inp):
        C, n, m = carry
        cl, nl, ml, gl = inp
        m_new = jnp.maximum(gl + m, ml)
        s_old = jnp.exp(gl + m - m_new)
        s_new = jnp.exp(ml - m_new)
        C_new = C * s_old[..., None, None] + cl * s_new[..., None, None]
        n_new = n * s_old[..., None] + nl * s_new[..., None]
        return (C_new, n_new, m_new), (C, n, m)

    init = (jnp.zeros((Bsz, H, Dh, Dh), jnp.float32), jnp.zeros((Bsz, H, Dh), jnp.float32),
            jnp.full((Bsz, H), -jnp.inf, jnp.float32))
    _, (c_prev, n_prev, m_prev) = lax.scan(
        step, init,
        (c_loc.transpose(1, 0, 2, 3, 4), n_loc.transpose(1, 0, 2, 3),
         m_loc.transpose(2, 0, 1), g.transpose(2, 0, 1)))
    c_prev = c_prev.transpose(1, 0, 2, 3, 4)
    n_prev = n_prev.transpose(1, 0, 2, 3)
    m_prev = m_prev.transpose(1, 2, 0)
    mask = np.tril(np.ones((L, L), dtype=bool))
    dmat = jnp.where(mask, b[..., :, None] - b[..., None, :] + li[..., None, :], -jnp.inf)
    m_inter = b + m_prev[..., None]
    m_l = jnp.maximum(jnp.max(dmat, axis=-1), m_inter)
    wq = jnp.exp(dmat - m_l[..., None]) * jnp.einsum('bclhd,bcshd->bhcls', qc, kc)
    sc = jnp.exp(m_inter - m_l)
    num = jnp.einsum('bhcls,bcshd->bclhd', wq, vc) + jnp.einsum('bclhd,bched,bhcl->bclhe', qc, c_prev, sc)
    den = jnp.sum(wq, axis=-1) + jnp.einsum('bclhd,bchd,bhcl->bhcl', qc, n_prev, sc)
    den = jnp.maximum(jnp.abs(den), jnp.exp(-m_l)).transpose(0, 2, 3, 1)
    return (num / den[..., None]).reshape(Bsz, S, H, Dh)


def _mlstm_mixer(q, k, v, o, ig, fg, i_bias, f_bias, norm_g):
    B, S = q.shape[0], q.shape[1]
    shp = (B, S, MLSTM_HEADS, HEAD_DIM)
    q = q.reshape(shp).astype(jnp.float32)
    k = k.reshape(shp).astype(jnp.float32) * (HEAD_DIM ** -0.5)
    v = v.reshape(shp).astype(jnp.float32)
    li = ig.astype(jnp.float32).reshape(B, S, 2, MLSTM_HEADS) + i_bias.astype(jnp.float32)
    lf = jax.nn.log_sigmoid(fg.astype(jnp.float32).reshape(B, S, 2, MLSTM_HEADS)
                            + f_bias.astype(jnp.float32))
    h_f = _mlstm_scan(q, k, v, li[:, :, 0], lf[:, :, 0])
    h_b = _flip(_mlstm_scan(_flip(q), _flip(k), _flip(v), _flip(li[:, :, 1]), _flip(lf[:, :, 1])))
    h = _rmsnorm(h_f + h_b, norm_g.reshape(MLSTM_HEADS, HEAD_DIM)).reshape(B, S, MLSTM_WIDTH)
    return (jax.nn.sigmoid(o.astype(jnp.float32)) * h).astype(o.dtype)


def _expert_choice_ffn(x, router, w_gate, w_up, w_down):
    B, S, D = x.shape
    T = B * S
    cap = EC_CAPACITY * T // N_EXPERTS
    xf = x.reshape(T, D)
    aff = jax.nn.softmax((xf @ router).astype(jnp.float32), axis=-1)
    gate, idx = lax.top_k(aff.T, cap)
    xs = xf[idx]
    hid = jax.nn.silu(jnp.einsum('ecd,edf->ecf', xs, w_gate)) * jnp.einsum('ecd,edf->ecf', xs, w_up)
    ye = jnp.einsum('ecf,efd->ecd', hid, w_down) * gate[..., None].astype(x.dtype)
    out = jnp.zeros((T, D), x.dtype).at[idx.reshape(-1)].add(ye.reshape(-1, D))
    return out.reshape(B, S, D)


def _trunk(x, mix_norm, w_in, attn_q_norm, attn_k_norm, attn_sink, rel_bias,
           ssd_conv_w, ssd_conv_b, ssd_dt_bias, ssd_a_log, ssd_d, ssd_norm,
           mlstm_i_bias, mlstm_f_bias, mlstm_norm, w_out, ffn_norm, router, w_gate, w_up, w_down):
    B, S = x.shape[0], x.shape[1]
    offsets = [int(o) for o in np.cumsum(IN_SPLITS)[:-1]]
    for l in range(DEPTH):
        h = _rmsnorm(x, mix_norm[l])
        aq, ak, av, sz, sxbc, sdt, mq, mk, mv, mo, mi, mf = jnp.split(h @ w_in[l], offsets, axis=-1)
        ya = _window_attention(aq.reshape(B, S, ATT_HEADS, HEAD_DIM),
                               ak.reshape(B, S, ATT_KV_HEADS, HEAD_DIM),
                               av.reshape(B, S, ATT_KV_HEADS, HEAD_DIM),
                               attn_q_norm[l], attn_k_norm[l], attn_sink[l], rel_bias)
        yb = _ssd_mixer(sz, sxbc, sdt, ssd_conv_w[l], ssd_conv_b[l], ssd_dt_bias[l],
                        ssd_a_log[l], ssd_d[l], ssd_norm[l])
        yc = _mlstm_mixer(mq, mk, mv, mo, mi, mf, mlstm_i_bias[l], mlstm_f_bias[l], mlstm_norm[l])
        x = x + jnp.concatenate([ya, yb, yc], axis=-1) @ w_out[l]
        x = x + _expert_choice_ffn(_rmsnorm(x, ffn_norm[l]), router[l], w_gate[l], w_up[l], w_down[l])
    return x


def setup_inputs(seed: int = 0) -> dict:
    key = jax.random.key(seed)
    ks = jax.random.split(key, 26)

    def nrm(k, shape, scale):
        return jax.random.normal(k, shape, jnp.float32) * scale

    dt0 = jnp.exp(jax.random.uniform(ks[10], (DEPTH, 2, SSD_HEADS), jnp.float32,
                                     minval=float(np.log(1e-3)), maxval=float(np.log(1e-1))))
    return {
        'x_prompt': nrm(ks[0], (BATCH, SEQ, D_MODEL), 1.0),
        'x_sample': nrm(ks[1], (DEC_BATCH, DEC_SEQ, D_MODEL), 1.0),
        'mix_norm': 1.0 + nrm(ks[2], (DEPTH, D_MODEL), 0.02),
        'w_in': nrm(ks[3], (DEPTH, D_MODEL, D_IN), D_MODEL ** -0.5),
        'attn_q_norm': 1.0 + nrm(ks[4], (DEPTH, HEAD_DIM), 0.02),
        'attn_k_norm': 1.0 + nrm(ks[5], (DEPTH, HEAD_DIM), 0.02),
        'attn_sink': nrm(ks[6], (DEPTH, ATT_HEADS), 0.5),
        'rel_bias': nrm(ks[7], (REL_BUCKETS, ATT_HEADS), 0.5),
        'ssd_conv_w': nrm(ks[8], (DEPTH, SSD_CONV, SSD_CONV_DIM), SSD_CONV ** -0.5),
        'ssd_conv_b': nrm(ks[9], (DEPTH, SSD_CONV_DIM), 0.02),
        'ssd_dt_bias': dt0 + jnp.log(-jnp.expm1(-dt0)),
        'ssd_a_log': jnp.log(jax.random.uniform(ks[11], (DEPTH, 2, SSD_HEADS), jnp.float32,
                                                minval=1.0, maxval=16.0)),
        'ssd_d': 1.0 + nrm(ks[12], (DEPTH, SSD_HEADS), 0.1),
        'ssd_norm': 1.0 + nrm(ks[13], (DEPTH, SSD_WIDTH), 0.02),
        'mlstm_i_bias': nrm(ks[14], (DEPTH, 2, MLSTM_HEADS), 0.1),
        'mlstm_f_bias': jnp.linspace(3.0, 6.0, MLSTM_HEADS, dtype=jnp.float32)
                        + nrm(ks[15], (DEPTH, 2, MLSTM_HEADS), 0.1),
        'mlstm_norm': 1.0 + nrm(ks[16], (DEPTH, MLSTM_WIDTH), 0.02),
        'w_out': nrm(ks[17], (DEPTH, D_MIX, D_MODEL), D_MIX ** -0.5),
        'ffn_norm': 1.0 + nrm(ks[18], (DEPTH, D_MODEL), 0.02),
        'router': nrm(ks[19], (DEPTH, D_MODEL, N_EXPERTS), D_MODEL ** -0.5),
        'w_gate': nrm(ks[20], (DEPTH, N_EXPERTS, D_MODEL, EXPERT_FF), D_MODEL ** -0.5),
        'w_up': nrm(ks[21], (DEPTH, N_EXPERTS, D_MODEL, EXPERT_FF), D_MODEL ** -0.5),
        'w_down': nrm(ks[22], (DEPTH, N_EXPERTS, EXPERT_FF, D_MODEL), EXPERT_FF ** -0.5),
    }


def reference(x_prompt, x_sample, mix_norm, w_in, attn_q_norm, attn_k_norm, attn_sink, rel_bias,
              ssd_conv_w, ssd_conv_b, ssd_dt_bias, ssd_a_log, ssd_d, ssd_norm,
              mlstm_i_bias, mlstm_f_bias, mlstm_norm, w_out, ffn_norm, router, w_gate, w_up, w_down):
    y_prompt = _trunk(x_prompt, mix_norm, w_in, attn_q_norm, attn_k_norm, attn_sink, rel_bias,
                      ssd_conv_w, ssd_conv_b, ssd_dt_bias, ssd_a_log, ssd_d, ssd_norm,
                      mlstm_i_bias, mlstm_f_bias, mlstm_norm, w_out, ffn_norm, router, w_gate, w_up, w_down)
    y_sample = _trunk(x_sample, mix_norm, w_in, attn_q_norm, attn_k_norm, attn_sink, rel_bias,
                      ssd_conv_w, ssd_conv_b, ssd_dt_bias, ssd_a_log, ssd_d, ssd_norm,
                      mlstm_i_bias, mlstm_f_bias, mlstm_norm, w_out, ffn_norm, router, w_gate, w_up, w_down)
    return (y_prompt, y_sample)
```

```python
import functools

import numpy as np
import jax
import jax.numpy as jnp
from jax import lax
from jax.experimental import pallas as pl
from jax.experimental.pallas import tpu as pltpu

F32 = jnp.float32
BF16 = jnp.bfloat16
NEG_INF = float("-inf")

LANES = 128
BF16_ROWS = 16
VMEM_LIMIT = 52 * 1024 * 1024

D_MODEL = 1024
HEAD_DIM = 64
ATT_HEADS = 4
ATT_KV_HEADS = 2
ATT_WIDTH = ATT_HEADS * HEAD_DIM
ATT_KV_WIDTH = ATT_KV_HEADS * HEAD_DIM
BLOCK = 128
WINDOW = 128
REL_BUCKETS = 32
REL_MAX_DIST = 128
SSD_WIDTH = 512
SSD_HEADS = 8
SSD_STATE = 64
SSD_GROUPS = 2
SSD_CONV = 5
SSD_BC = SSD_GROUPS * SSD_STATE
SSD_CONV_DIM = SSD_WIDTH + 2 * SSD_BC
MLSTM_WIDTH = 256
MLSTM_HEADS = 4
CHUNK = 128
N_EXPERTS = 16
EC_CAPACITY = 2
EXPERT_FF = 512
RMS_EPS = 1e-6
IN_SPLITS = (ATT_WIDTH, ATT_KV_WIDTH, ATT_KV_WIDTH, SSD_WIDTH, SSD_CONV_DIM, 2 * SSD_HEADS,
             MLSTM_WIDTH, MLSTM_WIDTH, MLSTM_WIDTH, MLSTM_WIDTH, 2 * MLSTM_HEADS, 2 * MLSTM_HEADS)

GATE_DT = 0
GATE_LI = 2 * SSD_HEADS
GATE_LF = GATE_LI + 2 * MLSTM_HEADS

TM_PROJ = 512
TQ_ATT = 512
T_CONV = 512
CONV_HALO = 16
TT_GATHER = 512
SLOT_WIN = 128
FFN_ROWS = 256
TT_COMB = 256
COMB_WIN = TT_COMB + BF16_ROWS


def _cparams(sem):
    return pltpu.CompilerParams(dimension_semantics=sem, vmem_limit_bytes=VMEM_LIMIT)


def _dot(a, b):
    return jnp.dot(a, b, preferred_element_type=F32)


def _dot_nt(a, b):
    return lax.dot_general(a, b, (((1,), (1,)), ((), ())), preferred_element_type=F32)


def _dot_tn(a, b):
    return lax.dot_general(a, b, (((0,), (0,)), ((), ())), preferred_element_type=F32)


def _split_bf16(x, n):
    parts = []
    r = x
    for i in range(n):
        p = r.astype(BF16)
        parts.append(p)
        if i + 1 < n:
            r = r - p.astype(F32)
    return parts


def _dot_hp(x, m, n=3):
    acc = None
    for p in _split_bf16(x, n):
        t = _dot(p, m)
        acc = t if acc is None else acc + t
    return acc


def _hp_dot(m, x, n=3):
    acc = None
    for p in _split_bf16(x, n):
        t = _dot(m, p)
        acc = t if acc is None else acc + t
    return acc


def _softplus(x):
    return jnp.maximum(x, 0.0) + jnp.log1p(jnp.exp(-jnp.abs(x)))


def _log_sigmoid(x):
    return -_softplus(-x)


def _sigmoid(x):
    return 1.0 / (1.0 + jnp.exp(-x))


def _silu(x):
    return x * _sigmoid(x)


def _inproj_kernel(x_ref, g_ref, watt_ref, wxbc_ref, wz_ref, wml_ref, wo_ref, wg_ref,
                   qg_ref, kg_ref, bdq_ref, bdk_ref,
                   att_ref, xbc_ref, z_ref, ml_ref, mo_ref, gate_ref, gatet_ref):
    x = x_ref[...]
    ms = jnp.mean(x * x, axis=-1, keepdims=True)
    h = (x * lax.rsqrt(ms + RMS_EPS) * g_ref[...]).astype(BF16)

    att = _dot(h, watt_ref[...])
    q = att[:, :ATT_WIDTH]
    k = att[:, ATT_WIDTH:ATT_WIDTH + ATT_KV_WIDTH]
    v = att[:, ATT_WIDTH + ATT_KV_WIDTH:]
    qms = _dot_hp(q * q, bdq_ref[...], 2)
    kms = _dot_hp(k * k, bdk_ref[...], 2)
    att_ref[:, :ATT_WIDTH] = (q * lax.rsqrt(qms + RMS_EPS) * qg_ref[...] * (HEAD_DIM ** -0.5)).astype(BF16)
    att_ref[:, ATT_WIDTH:ATT_WIDTH + ATT_KV_WIDTH] = (k * lax.rsqrt(kms + RMS_EPS) * kg_ref[...]).astype(BF16)
    att_ref[:, ATT_WIDTH + ATT_KV_WIDTH:] = v.astype(BF16)

    xbc_ref[...] = _dot(h, wxbc_ref[...]).astype(BF16)
    z_ref[...] = _dot(h, wz_ref[...]).astype(BF16)
    ml = _dot(h, wml_ref[...])
    ml_ref[:, :MLSTM_WIDTH] = ml[:, :MLSTM_WIDTH].astype(BF16)
    ml_ref[:, MLSTM_WIDTH:2 * MLSTM_WIDTH] = (ml[:, MLSTM_WIDTH:2 * MLSTM_WIDTH] * (HEAD_DIM ** -0.5)).astype(BF16)
    ml_ref[:, 2 * MLSTM_WIDTH:] = ml[:, 2 * MLSTM_WIDTH:].astype(BF16)
    mo_ref[...] = _dot(h, wo_ref[...]).astype(BF16)
    gates = _dot(h, wg_ref[...])
    gate_ref[...] = gates
    gatet_ref[...] = gates.T


def _inproj(x, p):
    t = x.shape[0]
    tm = TM_PROJ
    row = lambda w: pl.BlockSpec((tm, w), lambda i: (i, 0))
    full = lambda a: pl.BlockSpec(a.shape, lambda i: (0,) * a.ndim)
    ins = [p["mix_g"], p["w_att"], p["w_xbc"], p["w_z"], p["w_ml"], p["w_o"], p["w_g"],
           p["q_gain"], p["k_gain"], p["bd_q"], p["bd_k"]]
    out_shape = (
        jax.ShapeDtypeStruct((t, ATT_WIDTH + 2 * ATT_KV_WIDTH), BF16),
        jax.ShapeDtypeStruct((t, SSD_CONV_DIM), BF16),
        jax.ShapeDtypeStruct((t, SSD_WIDTH), BF16),
        jax.ShapeDtypeStruct((t, 3 * MLSTM_WIDTH), BF16),
        jax.ShapeDtypeStruct((t, MLSTM_WIDTH), BF16),
        jax.ShapeDtypeStruct((t, LANES), F32),
        jax.ShapeDtypeStruct((LANES, t), F32),
    )
    out_specs = (row(ATT_WIDTH + 2 * ATT_KV_WIDTH), row(SSD_CONV_DIM), row(SSD_WIDTH),
                 row(3 * MLSTM_WIDTH), row(MLSTM_WIDTH), row(LANES),
                 pl.BlockSpec((LANES, tm), lambda i: (0, i)))
    return pl.pallas_call(
        _inproj_kernel,
        grid=(t // tm,),
        in_specs=[row(D_MODEL)] + [full(a) for a in ins],
        out_specs=out_specs,
        out_shape=out_shape,
        compiler_params=_cparams(("parallel",)),
        name="inproj",
    )(x, *ins)


def _attn_kernel(sink_ref, prev_ref, cur_ref, next_ref, bias_ref, out_ref, *, nblk_total):
    j = pl.program_id(1)
    nq = TQ_ATT // BLOCK
    kv0 = ATT_WIDTH
    kv_all = jnp.concatenate([prev_ref[0][:, kv0:], cur_ref[0][:, kv0:], next_ref[0][:, kv0:]], axis=0)
    q_all = cur_ref[0][:, :ATT_WIDTH]
    col = lax.broadcasted_iota(jnp.int32, (1, 3 * BLOCK), 1)
    row = lax.broadcasted_iota(jnp.int32, (2 * BLOCK, 1), 0)
    for i in range(nq):
        blk = j * nq + i
        outside = ((blk == 0) & (col < BLOCK)) | ((blk == nblk_total - 1) & (col >= 2 * BLOCK))
        kv = kv_all[i * BLOCK:(i + 3) * BLOCK]
        q = q_all[i * BLOCK:(i + 1) * BLOCK]
        outs = []
        for g in range(ATT_KV_HEADS):
            h0 = 2 * g
            qs = jnp.concatenate([q[:, h0 * HEAD_DIM:(h0 + 1) * HEAD_DIM],
                                  q[:, (h0 + 1) * HEAD_DIM:(h0 + 2) * HEAD_DIM]], axis=0)
            kg = kv[:, g * HEAD_DIM:(g + 1) * HEAD_DIM]
            vg = kv[:, ATT_KV_WIDTH + g * HEAD_DIM:ATT_KV_WIDTH + (g + 1) * HEAD_DIM]
            s = _dot_nt(qs, kg) + bias_ref[g]
            s = jnp.where(outside, NEG_INF, s)
            sink = jnp.where(row < BLOCK, sink_ref[h0], sink_ref[h0 + 1])
            m = jnp.maximum(jnp.max(s, axis=-1, keepdims=True), sink)
            pexp = jnp.exp(s - m)
            denom = jnp.sum(pexp, axis=-1, keepdims=True) + jnp.exp(sink - m)
            o = _dot((pexp / denom).astype(BF16), vg)
            outs += [o[:BLOCK], o[BLOCK:]]
        out_ref[0, i * BLOCK:(i + 1) * BLOCK, :] = jnp.concatenate(outs, axis=1).astype(BF16)


def _attention(att, sink, bias, bsz, seq):
    w = ATT_WIDTH + 2 * ATT_KV_WIDTH
    a3 = att.reshape(bsz, seq, w)
    nq = TQ_ATT // BLOCK
    nblk = seq // BLOCK
    kern = functools.partial(_attn_kernel, nblk_total=nblk)
    out = pl.pallas_call(
        kern,
        grid=(bsz, seq // TQ_ATT),
        in_specs=[
            pl.BlockSpec(memory_space=pltpu.SMEM),
            pl.BlockSpec((1, BLOCK, w), lambda b, j: (b, jnp.maximum(j * nq - 1, 0), 0)),
            pl.BlockSpec((1, TQ_ATT, w), lambda b, j: (b, j, 0)),
            pl.BlockSpec((1, BLOCK, w), lambda b, j: (b, jnp.minimum(j * nq + nq, nblk - 1), 0)),
            pl.BlockSpec(bias.shape, lambda b, j: (0, 0, 0)),
        ],
        out_specs=pl.BlockSpec((1, TQ_ATT, ATT_WIDTH), lambda b, j: (b, j, 0)),
        out_shape=jax.ShapeDtypeStruct((bsz, seq, ATT_WIDTH), BF16),
        compiler_params=_cparams(("parallel", "parallel")),
        name="window_attn",
    )(sink, a3, a3, a3, bias)
    return out.reshape(bsz * seq, ATT_WIDTH)


def _conv_kernel(prev_ref, cur_ref, next_ref, w_ref, b_ref, out_ref, buf_ref):
    j = pl.program_id(1)
    nj = pl.num_programs(1)
    h = CONV_HALO
    buf_ref[0:h, :] = jnp.where(j > 0, prev_ref[0].astype(F32), 0.0)
    buf_ref[h:h + T_CONV, :] = cur_ref[0].astype(F32)
    buf_ref[h + T_CONV:, :] = jnp.where(j < nj - 1, next_ref[0].astype(F32), 0.0)
    acc = jnp.broadcast_to(b_ref[...], (T_CONV, SSD_CONV_DIM))
    half = SSD_CONV // 2
    for t in range(SSD_CONV):
        acc = acc + buf_ref[h - half + t:h - half + t + T_CONV, :] * w_ref[t:t + 1, :]
    out_ref[0] = _silu(acc).astype(BF16)


def _ssd_conv(xbc, conv_w, conv_b, bsz, seq):
    x3 = xbc.reshape(bsz, seq, SSD_CONV_DIM)
    r = T_CONV // CONV_HALO
    nh = seq // CONV_HALO
    out = pl.pallas_call(
        _conv_kernel,
        grid=(bsz, seq // T_CONV),
        in_specs=[
            pl.BlockSpec((1, CONV_HALO, SSD_CONV_DIM), lambda b, j: (b, jnp.maximum(j * r - 1, 0), 0)),
            pl.BlockSpec((1, T_CONV, SSD_CONV_DIM), lambda b, j: (b, j, 0)),
            pl.BlockSpec((1, CONV_HALO, SSD_CONV_DIM), lambda b, j: (b, jnp.minimum(j * r + r, nh - 1), 0)),
            pl.BlockSpec(conv_w.shape, lambda b, j: (0, 0)),
            pl.BlockSpec(conv_b.shape, lambda b, j: (0, 0)),
        ],
        out_specs=pl.BlockSpec((1, T_CONV, SSD_CONV_DIM), lambda b, j: (b, j, 0)),
        out_shape=jax.ShapeDtypeStruct((bsz, seq, SSD_CONV_DIM), BF16),
        scratch_shapes=[pltpu.VMEM((T_CONV + 2 * CONV_HALO, SSD_CONV_DIM), F32)],
        compiler_params=_cparams(("parallel", "parallel")),
        name="ssd_conv",
    )(x3, x3, x3, conv_w, conv_b)
    return out.reshape(bsz * seq, SSD_CONV_DIM)


def _ssd_direction(xc, gcol, grow, dtb_row, a_row, dtb_col, a_col, tri_l, tri_r, mask, expand,
                   col0, tot_idx, state_ref, out_ref):
    dt_c = _softplus(gcol + dtb_row)
    cs_c = _hp_dot(tri_l, dt_c * a_row)
    dt_r = _softplus(grow + dtb_col)
    cs_r = _dot_hp(dt_r * a_col, tri_r)
    tot = cs_c[tot_idx:tot_idx + 1, :]

    xs = xc[:, :SSD_WIDTH].astype(F32)
    bm = xc[:, SSD_WIDTH:SSD_WIDTH + SSD_BC]
    cm = xc[:, SSD_WIDTH + SSD_BC:]
    xdt = xs * _dot_hp(dt_c, expand)
    csx = _dot_hp(cs_c, expand)
    totx = _dot_hp(jnp.broadcast_to(tot, (8, LANES)), expand)[0:1, :]
    xdt_b = xdt.astype(BF16)
    xdec = (xdt * jnp.exp(totx - csx)).astype(BF16)
    e_in = jnp.exp(csx)
    chunk_dec = jnp.exp(totx)

    gw = SSD_WIDTH // SSD_GROUPS
    hpg = SSD_HEADS // SSD_GROUPS
    for g in range(SSD_GROUPS):
        bg = bm[:, g * SSD_STATE:(g + 1) * SSD_STATE]
        cg = cm[:, g * SSD_STATE:(g + 1) * SSD_STATE]
        cb = _dot_nt(cg, bg)
        sg = state_ref[g]
        y_off = _dot(cg, sg.astype(BF16)) * e_in[:, g * gw:(g + 1) * gw]
        for hh in range(hpg):
            hd = g * hpg + hh
            c = col0 + hd
            diff = cs_c[:, c:c + 1] - cs_r[c:c + 1, :]
            lmat = jnp.exp(jnp.where(mask, diff, NEG_INF))
            yd = _dot((cb * lmat).astype(BF16), xdt_b[:, hd * HEAD_DIM:(hd + 1) * HEAD_DIM])
            out_ref[0, :, hd * HEAD_DIM:(hd + 1) * HEAD_DIM] = yd + y_off[:, hh * HEAD_DIM:(hh + 1) * HEAD_DIM]
        st = _dot_tn(bg, xdec[:, g * gw:(g + 1) * gw])
        state_ref[g] = sg * chunk_dec[:, g * gw:(g + 1) * gw] + st


def _ssd_kernel(xf_ref, xb_ref, gf_ref, gb_ref, gtf_ref, gtb_ref,
                dtb_row_ref, a_row_ref, dtb_col_ref, a_col_ref, up_ref, lo_ref, ef_ref, eb_ref,
                yf_ref, yb_ref, sf_ref, sb_ref):
    c = pl.program_id(1)

    @pl.when(c == 0)
    def _():
        sf_ref[...] = jnp.zeros_like(sf_ref)
        sb_ref[...] = jnp.zeros_like(sb_ref)

    r = lax.broadcasted_iota(jnp.int32, (CHUNK, CHUNK), 0)
    s = lax.broadcasted_iota(jnp.int32, (CHUNK, CHUNK), 1)
    up = up_ref[...]
    lo = lo_ref[...]
    _ssd_direction(xf_ref[0], gf_ref[0], gtf_ref[...], dtb_row_ref[...], a_row_ref[...], dtb_col_ref[...],
                   a_col_ref[...], lo, up, s <= r, ef_ref[...], 0, CHUNK - 1, sf_ref, yf_ref)
    _ssd_direction(xb_ref[0], gb_ref[0], gtb_ref[...], dtb_row_ref[...], a_row_ref[...], dtb_col_ref[...],
                   a_col_ref[...], up, lo, s >= r, eb_ref[...], SSD_HEADS, 0, sb_ref, yb_ref)


def _ssd_scan(xc, gates, gates_t, p, bsz, seq):
    nc = seq // CHUNK
    x3 = xc.reshape(bsz, seq, SSD_CONV_DIM)
    g3 = gates.reshape(bsz, seq, LANES)
    fwd3 = lambda w: pl.BlockSpec((1, CHUNK, w), lambda b, c: (b, c, 0))
    bwd3 = lambda w: pl.BlockSpec((1, CHUNK, w), lambda b, c: (b, nc - 1 - c, 0))
    full = lambda a: pl.BlockSpec(a.shape, lambda b, c: (0,) * a.ndim)
    consts = [p["dtb_row"], p["a_row"], p["dtb_col"], p["a_col"], p["tri_up"], p["tri_lo"],
              p["ssd_ef"], p["ssd_eb"]]
    yshape = jax.ShapeDtypeStruct((bsz, seq, SSD_WIDTH), F32)
    st = pltpu.VMEM((SSD_GROUPS, SSD_STATE, SSD_WIDTH // SSD_GROUPS), F32)
    yf, yb = pl.pallas_call(
        _ssd_kernel,
        grid=(bsz, nc),
        in_specs=[fwd3(SSD_CONV_DIM), bwd3(SSD_CONV_DIM), fwd3(LANES), bwd3(LANES),
                  pl.BlockSpec((LANES, CHUNK), lambda b, c: (0, b * nc + c)),
                  pl.BlockSpec((LANES, CHUNK), lambda b, c: (0, b * nc + nc - 1 - c))]
                 + [full(a) for a in consts],
        out_specs=(fwd3(SSD_WIDTH), bwd3(SSD_WIDTH)),
        out_shape=(yshape, yshape),
        scratch_shapes=[st, st],
        compiler_params=_cparams(("parallel", "arbitrary")),
        name="ssd_scan",
    )(x3, x3, g3, g3, gates_t, gates_t, *consts)
    return yf.reshape(bsz * seq, SSD_WIDTH), yb.reshape(bsz * seq, SSD_WIDTH)


def _mlstm_direction(qkv, gcol, grow, ib_row, fb_row, ib_col, fb_col, tri_l, tri_r, mask, ones_col,
                     d, tot_idx, state_ref, m_ref, out_ref):
    li_c = gcol + ib_row
    li_r = grow + ib_col
    b_c = _hp_dot(tri_l, _log_sigmoid(gcol + fb_row))
    b_r = _dot_hp(_log_sigmoid(grow + fb_col), tri_r)
    for hd in range(MLSTM_HEADS):
        ci = GATE_LI + d * MLSTM_HEADS + hd
        cf = GATE_LF + d * MLSTM_HEADS + hd
        bc = b_c[:, cf:cf + 1]
        br = b_r[cf:cf + 1, :]
        lic = li_c[:, ci:ci + 1]
        lir = li_r[ci:ci + 1, :]
        g = bc[tot_idx:tot_idx + 1, :]
        m_prev = m_ref[d, hd][0:1, 0:1]
        st = state_ref[d, hd]
        qh = qkv[:, hd * HEAD_DIM:(hd + 1) * HEAD_DIM]
        kh = qkv[:, MLSTM_WIDTH + hd * HEAD_DIM:MLSTM_WIDTH + (hd + 1) * HEAD_DIM]
        vh = qkv[:, 2 * MLSTM_WIDTH + hd * HEAD_DIM:2 * MLSTM_WIDTH + (hd + 1) * HEAD_DIM]
        vext = jnp.concatenate([vh, ones_col], axis=1)

        dmat = jnp.where(mask, bc - br + lir, NEG_INF)
        m_inter = bc + m_prev
        m_l = jnp.maximum(jnp.max(dmat, axis=-1, keepdims=True), m_inter)
        wq = (jnp.exp(dmat - m_l) * _dot_nt(qh, kh)).astype(BF16)
        sc = jnp.exp(m_inter - m_l)
        tot = _dot(wq, vext) + sc * _dot(qh, st.astype(BF16))
        den = jnp.maximum(jnp.abs(tot[:, HEAD_DIM:HEAD_DIM + 1]), jnp.exp(-m_l))
        out_ref[0, :, hd * HEAD_DIM:(hd + 1) * HEAD_DIM] = tot[:, :HEAD_DIM] / den

        a = g - bc + lic
        m_loc = jnp.max(a, axis=0, keepdims=True)
        w = jnp.exp(a - m_loc)
        m_new = jnp.maximum(g + m_prev, m_loc)
        s_old = jnp.exp(g + m_prev - m_new)
        s_new = jnp.exp(m_loc - m_new)
        kw = (kh.astype(F32) * w).astype(BF16)
        state_ref[d, hd] = st * s_old + _dot_tn(kw, vext) * s_new
        m_ref[d, hd] = jnp.broadcast_to(m_new, (8, LANES))


def _mlstm_kernel(qf_ref, qb_ref, gf_ref, gb_ref, gtf_ref, gtb_ref,
                  ib_row_ref, fb_row_ref, ib_col_ref, fb_col_ref, up_ref, lo_ref,
                  hf_ref, hb_ref, state_ref, m_ref):
    c = pl.program_id(1)

    @pl.when(c == 0)
    def _():
        state_ref[...] = jnp.zeros_like(state_ref)
        m_ref[...] = jnp.full(m_ref.shape, NEG_INF, F32)

    r = lax.broadcasted_iota(jnp.int32, (CHUNK, CHUNK), 0)
    s = lax.broadcasted_iota(jnp.int32, (CHUNK, CHUNK), 1)
    ones_col = (lax.broadcasted_iota(jnp.int32, (CHUNK, LANES - HEAD_DIM), 1) == 0).astype(BF16)
    up = up_ref[...]
    lo = lo_ref[...]
    common = (ib_row_ref[...], fb_row_ref[...], ib_col_ref[...], fb_col_ref[...])
    _mlstm_direction(qf_ref[0], gf_ref[0], gtf_ref[...], *common, lo, up, s <= r, ones_col,
                     0, CHUNK - 1, state_ref, m_ref, hf_ref)
    _mlstm_direction(qb_ref[0], gb_ref[0], gtb_ref[...], *common, up, lo, s >= r, ones_col,
                     1, 0, state_ref, m_ref, hb_ref)


def _mlstm_scan(ml, gates, gates_t, p, bsz, seq):
    nc = seq // CHUNK
    q3 = ml.reshape(bsz, seq, 3 * MLSTM_WIDTH)
    g3 = gates.reshape(bsz, seq, LANES)
    fwd3 = lambda w: pl.BlockSpec((1, CHUNK, w), lambda b, c: (b, c, 0))
    bwd3 = lambda w: pl.BlockSpec((1, CHUNK, w), lambda b, c: (b, nc - 1 - c, 0))
    full = lambda a: pl.BlockSpec(a.shape, lambda b, c: (0,) * a.ndim)
    consts = [p["ib_row"], p["fb_row"], p["ib_col"], p["fb_col"], p["tri_up"], p["tri_lo"]]
    hshape = jax.ShapeDtypeStruct((bsz, seq, MLSTM_WIDTH), F32)
    hf, hb = pl.pallas_call(
        _mlstm_kernel,
        grid=(bsz, nc),
        in_specs=[fwd3(3 * MLSTM_WIDTH), bwd3(3 * MLSTM_WIDTH), fwd3(LANES), bwd3(LANES),
                  pl.BlockSpec((LANES, CHUNK), lambda b, c: (0, b * nc + c)),
                  pl.BlockSpec((LANES, CHUNK), lambda b, c: (0, b * nc + nc - 1 - c))]
                 + [full(a) for a in consts],
        out_specs=(fwd3(MLSTM_WIDTH), bwd3(MLSTM_WIDTH)),
        out_shape=(hshape, hshape),
        scratch_shapes=[pltpu.VMEM((2, MLSTM_HEADS, HEAD_DIM, LANES), F32),
                        pltpu.VMEM((2, MLSTM_HEADS, 8, LANES), F32)],
        compiler_params=_cparams(("parallel", "arbitrary")),
        name="mlstm_scan",
    )(q3, q3, g3, g3, gates_t, gates_t, *consts)
    return hf.reshape(bsz * seq, MLSTM_WIDTH), hb.reshape(bsz * seq, MLSTM_WIDTH)


def _outproj_kernel(x_ref, ya_ref, yf_ref, yb_ref, xc_ref, z_ref, hf_ref, hb_ref, o_ref,
                    dskip_ref, sg_ref, mg_ref, bdm_ref, wa_ref, wb_ref, wc_ref, fg_ref, r_hi_ref, r_lo_ref,
                    x1_ref, xn_ref, aff_ref, afft_ref):
    xs = xc_ref[:, :SSD_WIDTH].astype(F32)
    y = (yf_ref[...] + yb_ref[...] + xs * dskip_ref[...]) * _silu(z_ref[...].astype(F32))
    yb = (y * lax.rsqrt(jnp.mean(y * y, axis=-1, keepdims=True) + RMS_EPS) * sg_ref[...]).astype(BF16)

    hm = hf_ref[...] + hb_ref[...]
    hms = _dot_hp(hm * hm, bdm_ref[...], 2)
    yc = (_sigmoid(o_ref[...].astype(F32)) * (hm * lax.rsqrt(hms + RMS_EPS) * mg_ref[...])).astype(BF16)

    x1 = x_ref[...] + _dot(ya_ref[...], wa_ref[...]) + _dot(yb, wb_ref[...]) + _dot(yc, wc_ref[...])
    x1_ref[...] = x1
    xn = x1 * lax.rsqrt(jnp.mean(x1 * x1, axis=-1, keepdims=True) + RMS_EPS) * fg_ref[...]
    xn_ref[...] = xn.astype(BF16)

    x_hi, x_lo = _split_bf16(xn, 2)
    logits = _dot(x_hi, r_hi_ref[...]) + (_dot(x_lo, r_hi_ref[...]) + _dot(x_hi, r_lo_ref[...]))
    lane = lax.broadcasted_iota(jnp.int32, (1, LANES), 1)
    logits = jnp.where(lane < N_EXPERTS, logits, NEG_INF)
    e = jnp.exp(logits - jnp.max(logits, axis=-1, keepdims=True))
    aff = e / jnp.sum(e, axis=-1, keepdims=True)
    aff_ref[...] = aff
    afft_ref[...] = aff.T[:N_EXPERTS, :]


def _outproj(x, ya, yf, yb, xc, z, hf, hb, mo, p):
    t = x.shape[0]
    tm = TM_PROJ
    row = lambda w: pl.BlockSpec((tm, w), lambda i: (i, 0))
    full = lambda a: pl.BlockSpec(a.shape, lambda i: (0,) * a.ndim)
    consts = [p["d_skip"], p["ssd_gain"], p["ml_gain"], p["bd_m"], p["w_out_a"], p["w_out_b"], p["w_out_c"],
              p["ffn_g"], p["r_hi"], p["r_lo"]]
    return pl.pallas_call(
        _outproj_kernel,
        grid=(t // tm,),
        in_specs=[row(D_MODEL), row(ATT_WIDTH), row(SSD_WIDTH), row(SSD_WIDTH), row(SSD_CONV_DIM),
                  row(SSD_WIDTH), row(MLSTM_WIDTH), row(MLSTM_WIDTH), row(MLSTM_WIDTH)]
                 + [full(a) for a in consts],
        out_specs=(row(D_MODEL), row(D_MODEL), row(LANES), pl.BlockSpec((N_EXPERTS, tm), lambda i: (0, i))),
        out_shape=(jax.ShapeDtypeStruct((t, D_MODEL), F32), jax.ShapeDtypeStruct((t, D_MODEL), BF16),
                   jax.ShapeDtypeStruct((t, LANES), F32), jax.ShapeDtypeStruct((N_EXPERTS, t), F32)),
        compiler_params=_cparams(("parallel",)),
        name="outproj",
    )(x, ya, yf, yb, xc, z, hf, hb, mo, *consts)


def _route_kernel(aff_ref, tri_ref, tri_blk_ref, pos_ref, off_ref, *, cap):
    key = pltpu.bitcast(aff_ref[...], jnp.int32)
    e, nb, _ = key.shape

    def count(pred):
        c = jnp.sum(jnp.where(pred, 1.0, 0.0), axis=2, keepdims=True)
        return jnp.sum(c, axis=1, keepdims=True)

    def vbit(i, thr):
        cand = thr | (jnp.int32(1) << (30 - i))
        return jnp.where(count(key >= cand) >= cap, cand, thr)

    thr = lax.fori_loop(0, 31, vbit, jnp.zeros((e, 1, 1), jnp.int32))
    above = key > thr
    tie = key == thr
    need = cap - count(above)
    tok = (lax.broadcasted_iota(jnp.int32, key.shape, 1) * LANES
           + lax.broadcasted_iota(jnp.int32, key.shape, 2))
    nbits = max(1, int(nb * LANES - 1).bit_length())

    def ibit(i, lim):
        cand = lim | (jnp.int32(1) << (nbits - 1 - i))
        return jnp.where(count(tie & (tok < cand)) < need, cand, lim)

    lim = lax.fori_loop(0, nbits, ibit, jnp.zeros((e, 1, 1), jnp.int32))
    chosen = above | (tie & (tok <= lim))
    sel = jnp.where(chosen, 1.0, 0.0).astype(BF16).reshape(e * nb, LANES)
    within = _dot(sel, tri_ref[...])
    for x in range(e):
        w_x = within[x * nb:(x + 1) * nb]
        before = _dot(tri_blk_ref[...], w_x.astype(BF16))[:, LANES - 1:LANES]
        pos_ref[x] = jnp.where(chosen[x], w_x + before - 1.0, -1.0).astype(jnp.int32)
        off_ref[x] = before.astype(jnp.int32)


def _route(aff_t, p, t):
    nb = t // LANES
    cap = EC_CAPACITY * t // N_EXPERTS
    a3 = aff_t.reshape(N_EXPERTS, nb, LANES)
    tri_blk = jnp.asarray(np.tril(np.ones((nb, nb), np.float32), -1), BF16)
    pos, off = pl.pallas_call(
        functools.partial(_route_kernel, cap=cap),
        out_shape=(jax.ShapeDtypeStruct((N_EXPERTS, nb, LANES), jnp.int32),
                   jax.ShapeDtypeStruct((N_EXPERTS, nb, 1), jnp.int32)),
        compiler_params=pltpu.CompilerParams(vmem_limit_bytes=VMEM_LIMIT),
        name="route",
    )(a3, p["tri_up"], tri_blk)
    return pos.reshape(N_EXPERTS, t), off.reshape(N_EXPERTS, nb)


def _ffn_kernel(off_ref, xn_ref, pos_ref, wg_ref, wu_ref, wd_ref, ye_ref, xs_ref, *, cap, blocks_per_tile):
    e = pl.program_id(0)
    j = pl.program_id(1)
    nj = pl.num_programs(1)

    @pl.when(j == 0)
    def _():
        xs_ref[...] = jnp.zeros_like(xs_ref)

    start = off_ref[e, j * blocks_per_tile]
    end = jnp.where(j == nj - 1, cap, off_ref[e, jnp.minimum((j + 1) * blocks_per_tile, nj * blocks_per_tile - 1)])
    xn = xn_ref[...]
    pos = pos_ref[0]
    w_lo = start // SLOT_WIN
    w_hi = (end + SLOT_WIN - 1) // SLOT_WIN

    def window(w, carry):
        base = pl.multiple_of(w * SLOT_WIN, SLOT_WIN)
        slot = base + lax.broadcasted_iota(jnp.int32, (SLOT_WIN, 1), 0)
        onehot = jnp.where(pos == slot, 1.0, 0.0).astype(BF16)
        rows = _dot(onehot, xn).astype(BF16)
        xs_ref[pl.ds(base, SLOT_WIN), :] = xs_ref[pl.ds(base, SLOT_WIN), :] + rows
        return carry

    lax.fori_loop(w_lo, w_hi, window, 0)

    @pl.when(j == nj - 1)
    def _():
        def chunk(c, carry):
            r0 = pl.multiple_of(c * FFN_ROWS, FFN_ROWS)
            xs = xs_ref[pl.ds(r0, FFN_ROWS), :]
            hid = (_silu(_dot(xs, wg_ref[0])) * _dot(xs, wu_ref[0])).astype(BF16)
            ye_ref[0, pl.ds(r0, FFN_ROWS), :] = _dot(hid, wd_ref[0]).astype(BF16)
            return carry

        lax.fori_loop(0, cap // FFN_ROWS, chunk, 0)
        ye_ref[0, cap:, :] = jnp.zeros((COMB_WIN, D_MODEL), BF16)


def _expert_ffn(xn, pos_t, off, wg, wu, wd, t):
    cap = EC_CAPACITY * t // N_EXPERTS
    tt = min(TT_GATHER, t)
    nj = t // tt
    pos3 = pos_t.reshape(N_EXPERTS, 1, t)
    kern = functools.partial(_ffn_kernel, cap=cap, blocks_per_tile=tt // LANES)
    grid_spec = pltpu.PrefetchScalarGridSpec(
        num_scalar_prefetch=1,
        grid=(N_EXPERTS, nj),
        in_specs=[
            pl.BlockSpec((tt, D_MODEL), lambda e, j, off: (j, 0)),
            pl.BlockSpec((1, 1, tt), lambda e, j, off: (e, 0, j)),
            pl.BlockSpec((1, D_MODEL, EXPERT_FF), lambda e, j, off: (e, 0, 0)),
            pl.BlockSpec((1, D_MODEL, EXPERT_FF), lambda e, j, off: (e, 0, 0)),
            pl.BlockSpec((1, EXPERT_FF, D_MODEL), lambda e, j, off: (e, 0, 0)),
        ],
        out_specs=pl.BlockSpec((1, cap + COMB_WIN, D_MODEL), lambda e, j, off: (e, 0, 0)),
        scratch_shapes=[pltpu.VMEM((cap, D_MODEL), BF16)],
    )
    return pl.pallas_call(
        kern,
        grid_spec=grid_spec,
        out_shape=jax.ShapeDtypeStruct((N_EXPERTS, cap + COMB_WIN, D_MODEL), BF16),
        compiler_params=_cparams(("parallel", "arbitrary")),
        name="expert_ffn",
    )(off, xn, pos3, wg, wu, wd)


def _combine_kernel(off_ref, x1_ref, pos_ref, gate_ref, ye_ref, out_ref, buf_ref, sem_ref, *, blocks_per_tile):
    j = pl.program_id(0)

    def copy(e):
        start = off_ref[e, j * blocks_per_tile]
        base = pl.multiple_of((start // BF16_ROWS) * BF16_ROWS, BF16_ROWS)
        return base, pltpu.make_async_copy(ye_ref.at[e, pl.ds(base, COMB_WIN), :], buf_ref.at[e], sem_ref.at[e])

    for e in range(N_EXPERTS):
        copy(e)[1].start()
    acc = x1_ref[...]
    pos = pos_ref[...]
    gate = gate_ref[...]
    for e in range(N_EXPERTS):
        base, cp = copy(e)
        cp.wait()
        slot = base + lax.broadcasted_iota(jnp.int32, (1, COMB_WIN), 1)
        onehot = jnp.where(pos[:, e:e + 1] == slot, 1.0, 0.0).astype(BF16)
        acc = acc + gate[:, e:e + 1] * _dot(onehot, buf_ref[e])
    out_ref[...] = acc


def _combine(x1, pos_tok, aff, off, ye, t):
    tt = min(TT_COMB, t)
    grid_spec = pltpu.PrefetchScalarGridSpec(
        num_scalar_prefetch=1,
        grid=(t // tt,),
        in_specs=[
            pl.BlockSpec((tt, D_MODEL), lambda j, off: (j, 0)),
            pl.BlockSpec((tt, LANES), lambda j, off: (j, 0)),
            pl.BlockSpec((tt, LANES), lambda j, off: (j, 0)),
            pl.BlockSpec(memory_space=pl.ANY),
        ],
        out_specs=pl.BlockSpec((tt, D_MODEL), lambda j, off: (j, 0)),
        scratch_shapes=[pltpu.VMEM((N_EXPERTS, COMB_WIN, D_MODEL), BF16),
                        pltpu.SemaphoreType.DMA((N_EXPERTS,))],
    )
    return pl.pallas_call(
        functools.partial(_combine_kernel, blocks_per_tile=tt // LANES),
        grid_spec=grid_spec,
        out_shape=jax.ShapeDtypeStruct((t, D_MODEL), F32),
        compiler_params=_cparams(("arbitrary",)),
        name="combine",
    )(off, x1, pos_tok, aff, ye)


def _t5_buckets(rel):
    half = REL_BUCKETS // 2
    exact = half // 2
    n = np.abs(rel)
    large = exact + (np.log(np.maximum(n, 1) / exact) / np.log(REL_MAX_DIST / exact)
                     * (half - exact)).astype(np.int32)
    large = np.minimum(large, half - 1)
    return (rel > 0).astype(np.int32) * half + np.where(n < exact, n, large)


def _block_diag_mean(width):
    m = np.kron(np.eye(width // HEAD_DIM, dtype=np.float32), np.full((HEAD_DIM, HEAD_DIM), 1.0 / HEAD_DIM, np.float32))
    return jnp.asarray(m, BF16)


def _pad_row(v, offset):
    return jnp.zeros((1, LANES), F32).at[0, offset:offset + v.shape[0]].set(v.astype(F32))


def _expand_matrix(col0):
    m = np.zeros((LANES, SSD_WIDTH), np.float32)
    for h in range(SSD_HEADS):
        m[col0 + h, h * HEAD_DIM:(h + 1) * HEAD_DIM] = 1.0
    return jnp.asarray(m, BF16)


def _shared_tables(rel_bias):
    kpos = np.arange(3 * BLOCK)[None, :] - BLOCK
    rel = kpos - np.arange(BLOCK)[:, None]
    bias = rel_bias.astype(F32)[_t5_buckets(rel)]
    bias = jnp.where((np.abs(rel) <= WINDOW)[:, :, None], bias, NEG_INF)
    bias = bias.transpose(2, 0, 1).reshape(ATT_KV_HEADS, 2 * BLOCK, 3 * BLOCK)
    up = np.triu(np.ones((CHUNK, CHUNK), np.float32))
    return {
        "att_bias": bias,
        "tri_up": jnp.asarray(up, BF16),
        "tri_lo": jnp.asarray(up.T, BF16),
        "ssd_ef": _expand_matrix(0),
        "ssd_eb": _expand_matrix(SSD_HEADS),
        "bd_q": _block_diag_mean(ATT_WIDTH),
        "bd_k": _block_diag_mean(ATT_KV_WIDTH),
        "bd_m": _block_diag_mean(MLSTM_WIDTH),
    }


def _layer_params(l, shared, mix_norm, w_in, attn_q_norm, attn_k_norm, attn_sink, ssd_conv_w, ssd_conv_b,
                  ssd_dt_bias, ssd_a_log, ssd_d, ssd_norm, mlstm_i_bias, mlstm_f_bias, mlstm_norm, w_out,
                  ffn_norm, router, w_gate, w_up, w_down):
    offs = np.concatenate([[0], np.cumsum(IN_SPLITS)])
    w = w_in[l]
    cols = lambda a, b: w[:, int(offs[a]):int(offs[b])]
    w_gates = jnp.zeros((D_MODEL, LANES), F32)
    w_gates = w_gates.at[:, GATE_DT:GATE_DT + 2 * SSD_HEADS].set(cols(5, 6))
    w_gates = w_gates.at[:, GATE_LI:GATE_LI + 2 * MLSTM_HEADS].set(cols(10, 11))
    w_gates = w_gates.at[:, GATE_LF:GATE_LF + 2 * MLSTM_HEADS].set(cols(11, 12))
    dtb = _pad_row(ssd_dt_bias[l].reshape(-1), GATE_DT)
    a_neg = _pad_row(-jnp.exp(ssd_a_log[l].astype(F32)).reshape(-1), GATE_DT)
    ib = _pad_row(mlstm_i_bias[l].reshape(-1), GATE_LI)
    fb = _pad_row(mlstm_f_bias[l].reshape(-1), GATE_LF)
    r_pad = jnp.zeros((D_MODEL, LANES), F32).at[:, :N_EXPERTS].set(router[l])
    r_hi = r_pad.astype(BF16)
    r_lo = (r_pad - r_hi.astype(F32)).astype(BF16)
    p = dict(shared)
    p.update({
        "mix_g": mix_norm[l].reshape(1, D_MODEL),
        "w_att": cols(0, 3).astype(BF16),
        "w_xbc": cols(4, 5).astype(BF16),
        "w_z": cols(3, 4).astype(BF16),
        "w_ml": cols(6, 9).astype(BF16),
        "w_o": cols(9, 10).astype(BF16),
        "w_g": w_gates.astype(BF16),
        "q_gain": jnp.tile(attn_q_norm[l], ATT_HEADS).reshape(1, ATT_WIDTH),
        "k_gain": jnp.tile(attn_k_norm[l], ATT_KV_HEADS).reshape(1, ATT_KV_WIDTH),
        "sink": attn_sink[l].astype(F32),
        "conv_w": jnp.zeros((8, SSD_CONV_DIM), F32).at[:SSD_CONV].set(ssd_conv_w[l]),
        "conv_b": ssd_conv_b[l].reshape(1, SSD_CONV_DIM),
        "dtb_row": dtb, "a_row": a_neg, "dtb_col": dtb.reshape(LANES, 1), "a_col": a_neg.reshape(LANES, 1),
        "ib_row": ib, "fb_row": fb, "ib_col": ib.reshape(LANES, 1), "fb_col": fb.reshape(LANES, 1),
        "d_skip": jnp.repeat(ssd_d[l].astype(F32), HEAD_DIM).reshape(1, SSD_WIDTH),
        "ssd_gain": ssd_norm[l].reshape(1, SSD_WIDTH),
        "ml_gain": mlstm_norm[l].reshape(1, MLSTM_WIDTH),
        "w_out_a": w_out[l][:ATT_WIDTH].astype(BF16),
        "w_out_b": w_out[l][ATT_WIDTH:ATT_WIDTH + SSD_WIDTH].astype(BF16),
        "w_out_c": w_out[l][ATT_WIDTH + SSD_WIDTH:].astype(BF16),
        "ffn_g": ffn_norm[l].reshape(1, D_MODEL),
        "r_hi": r_hi, "r_lo": r_lo,
        "wg": w_gate[l].astype(BF16), "wu": w_up[l].astype(BF16), "wd": w_down[l].astype(BF16),
    })
    return p


def _layer(x, p, bsz, seq):
    t = bsz * seq
    att, xbc, z, ml, mo, gates, gates_t = _inproj(x, p)
    ya = _attention(att, p["sink"], p["att_bias"], bsz, seq)
    xc = _ssd_conv(xbc, p["conv_w"], p["conv_b"], bsz, seq)
    yf, yb = _ssd_scan(xc, gates, gates_t, p, bsz, seq)
    hf, hb = _mlstm_scan(ml, gates, gates_t, p, bsz, seq)
    x1, xn, aff, aff_t = _outproj(x, ya, yf, yb, xc, z, hf, hb, mo, p)
    pos_t, off = _route(aff_t, p, t)
    ye = _expert_ffn(xn, pos_t, off, p["wg"], p["wu"], p["wd"], t)
    pos_tok = jnp.full((t, LANES), -1, jnp.int32).at[:, :N_EXPERTS].set(pos_t.T)
    return _combine(x1, pos_tok, aff, off, ye, t)


def _trunk(x, layers):
    bsz, seq, _ = x.shape
    h = x.reshape(bsz * seq, D_MODEL)
    for p in layers:
        h = _layer(h, p, bsz, seq)
    return h.reshape(bsz, seq, D_MODEL)


def kernel(x_prompt, x_sample, mix_norm, w_in, attn_q_norm, attn_k_norm, attn_sink, rel_bias, ssd_conv_w, ssd_conv_b, ssd_dt_bias, ssd_a_log, ssd_d, ssd_norm, mlstm_i_bias, mlstm_f_bias, mlstm_norm, w_out, ffn_norm, router, w_gate, w_up, w_down):
    shared = _shared_tables(rel_bias)
    layers = [_layer_params(l, shared, mix_norm, w_in, attn_q_norm, attn_k_norm, attn_sink, ssd_conv_w,
                            ssd_conv_b, ssd_dt_bias, ssd_a_log, ssd_d, ssd_norm, mlstm_i_bias, mlstm_f_bias,
                            mlstm_norm, w_out, ffn_norm, router, w_gate, w_up, w_down)
              for l in range(w_in.shape[0])]
    return (_trunk(x_prompt, layers), _trunk(x_sample, layers))
```
